```python
import jax, jax.numpy as jnp
from jax import lax
import numpy as np

D_MODEL = 1024
BATCH = 2
SEQ = 16384
DEPTH = 4

MLA_HEADS = 4
QK_NOPE_DIM = 128
QK_ROPE_DIM = 64
QK_HEAD_DIM = QK_NOPE_DIM + QK_ROPE_DIM
V_HEAD_DIM = 128
Q_LORA_RANK = 256
KV_LORA_RANK = 128
ROPE_THETA = 10000.0
SB_HEADS = 4
SB_HEAD_DIM = 128
D_FF = 2816
CONV_WIDTH = 3
BLOCK_Q = 128
SUB = 16
N_SUB = BLOCK_Q // SUB
EPS = 1e-6
N_MOD = 6

MLA_WIDTH = MLA_HEADS * V_HEAD_DIM
SB_WIDTH = SB_HEADS * SB_HEAD_DIM
IN_SPLITS = (Q_LORA_RANK, KV_LORA_RANK, QK_ROPE_DIM, SB_WIDTH, SB_WIDTH, SB_WIDTH, D_MODEL, D_MODEL)
D_IN = sum(IN_SPLITS)
IN_SPLIT_POINTS = [sum(IN_SPLITS[:i + 1]) for i in range(len(IN_SPLITS) - 1)]

kernel_name = "hybrid_mla_stickbreak_convffn_adaln"


def rms_norm(x, g):
    x32 = x.astype(jnp.float32)
    y = x32 * lax.rsqrt(jnp.mean(x32 * x32, axis=-1, keepdims=True) + EPS)
    return (y * g.astype(jnp.float32)).astype(x.dtype)


def rope(x, positions):
    dr = x.shape[-1]
    half = dr // 2
    inv_freq = 1.0 / (ROPE_THETA ** (jnp.arange(half, dtype=jnp.float32) * (2.0 / dr)))
    ang = positions.astype(jnp.float32)[..., None] * inv_freq
    ang = ang.reshape(ang.shape[:2] + (1,) * (x.ndim - 3) + (half,))
    cos, sin = jnp.cos(ang), jnp.sin(ang)
    x32 = x.astype(jnp.float32)
    x1, x2 = x32[..., :half], x32[..., half:]
    return jnp.concatenate([x1 * cos - x2 * sin, x2 * cos + x1 * sin], axis=-1).astype(x.dtype)


def causal_softmax_attention(q, k, v):
    b, s, h, _ = q.shape
    scale = QK_HEAD_DIM ** -0.5
    qh = q.transpose(0, 2, 1, 3).astype(jnp.float32)
    kh = k.transpose(0, 2, 1, 3).astype(jnp.float32)
    vh = v.transpose(0, 2, 1, 3).astype(jnp.float32)
    outs = []
    for i in range(s // BLOCK_Q):
        st = i * BLOCK_Q
        ln = st + BLOCK_Q
        sc = jnp.einsum('bhqd,bhkd->bhqk', qh[:, :, st:ln], kh[:, :, :ln]) * scale
        mask = np.arange(ln)[None, :] <= np.arange(st, ln)[:, None]
        p = jax.nn.softmax(jnp.where(mask, sc, -jnp.inf), axis=-1)
        outs.append(jnp.einsum('bhqk,bhkd->bhqd', p, vh[:, :, :ln]))
    o = jnp.concatenate(outs, axis=2)
    return o.transpose(0, 2, 1, 3).reshape(b, s, h * v.shape[-1]).astype(v.dtype)


def reverse_cumsum_keys(l):
    lead = l.shape[:-1]
    nk = l.shape[-1] // BLOCK_Q
    r = l.reshape(lead + (nk, N_SUB, SUB))
    c0 = lax.cumsum(r, axis=r.ndim - 1, reverse=True)
    s0 = c0[..., 0]
    c1 = lax.cumsum(s0, axis=s0.ndim - 1, reverse=True)
    s1 = c1[..., 0]
    c2 = lax.cumsum(s1, axis=s1.ndim - 1, reverse=True) - s1
    tot = c0 + (c1 - s0)[..., None] + c2[..., None, None]
    return tot.reshape(l.shape)


def stick_breaking_attention(q, k, v):
    b, s, h, d = q.shape
    scale = d ** -0.5
    qh = q.transpose(0, 2, 1, 3).astype(jnp.float32)
    kh = k.transpose(0, 2, 1, 3).astype(jnp.float32)
    vh = v.transpose(0, 2, 1, 3).astype(jnp.float32)
    outs = []
    for i in range(s // BLOCK_Q):
        st = i * BLOCK_Q
        ln = st + BLOCK_Q
        z = jnp.einsum('bhqd,bhkd->bhqk', qh[:, :, st:ln], kh[:, :, :ln]) * scale
        past = np.arange(ln)[None, :] < np.arange(st, ln)[:, None]
        l = jnp.where(past, jax.nn.log_sigmoid(-z), 0.0)
        a = jnp.where(past, jnp.exp(z + reverse_cumsum_keys(l)), 0.0)
        outs.append(jnp.einsum('bhqk,bhkd->bhqd', a, vh[:, :, :ln]))
    o = jnp.concatenate(outs, axis=2)
    return o.transpose(0, 2, 1, 3).reshape(b, s, h * d).astype(v.dtype)


def causal_depthwise_conv(u, w, bias):
    ch = u.shape[-1]
    out = lax.conv_general_dilated(
        u, w[:, None, :].astype(u.dtype), window_strides=(1,),
        padding=[(CONV_WIDTH - 1, 0)], dimension_numbers=('NWC', 'WIO', 'NWC'),
        feature_group_count=ch)
    return out + bias


def setup_inputs(seed: int = 0) -> dict:
    key = jax.random.key(seed)
    ks = jax.random.split(key, 24)
    f32 = jnp.float32

    def nrm(k, shape, scale):
        return jax.random.normal(k, shape, f32) * scale

    def gain(k, n):
        return 1.0 + 0.02 * jax.random.normal(k, (DEPTH, n), f32)

    return {
        "x": nrm(ks[0], (BATCH, SEQ, D_MODEL), 1.0),
        "c": nrm(ks[1], (BATCH, D_MODEL), 1.0),
        "positions": jnp.broadcast_to(jnp.arange(SEQ, dtype=jnp.int32)[None, :], (BATCH, SEQ)),
        "w_ada": nrm(ks[2], (DEPTH, D_MODEL, N_MOD * D_MODEL), 0.5 * D_MODEL ** -0.5),
        "b_ada": nrm(ks[3], (DEPTH, N_MOD * D_MODEL), 0.02),
        "g_norm1": gain(ks[4], D_MODEL),
        "w_in": nrm(ks[5], (DEPTH, D_MODEL, D_IN), D_MODEL ** -0.5),
        "b_in": nrm(ks[6], (DEPTH, D_IN), 0.02),
        "g_q_lat": gain(ks[7], Q_LORA_RANK),
        "w_uq": nrm(ks[8], (DEPTH, Q_LORA_RANK, MLA_HEADS * QK_HEAD_DIM), Q_LORA_RANK ** -0.5),
        "g_kv_lat": gain(ks[9], KV_LORA_RANK),
        "w_ukv": nrm(ks[10], (DEPTH, KV_LORA_RANK, MLA_HEADS * (QK_NOPE_DIM + V_HEAD_DIM)), KV_LORA_RANK ** -0.5),
        "g_q_head": gain(ks[11], QK_HEAD_DIM),
        "g_k_head": gain(ks[12], QK_HEAD_DIM),
        "w_branch_mla": nrm(ks[13], (DEPTH, MLA_WIDTH, D_MODEL), MLA_WIDTH ** -0.5),
        "w_branch_sb": nrm(ks[14], (DEPTH, SB_WIDTH, D_MODEL), SB_WIDTH ** -0.5),
        "w_out": nrm(ks[15], (DEPTH, D_MODEL, D_MODEL), D_MODEL ** -0.5),
        "g_norm2": gain(ks[16], D_MODEL),
        "w_up": nrm(ks[17], (DEPTH, D_MODEL, 2 * D_FF), D_MODEL ** -0.5),
        "w_conv": nrm(ks[18], (DEPTH, CONV_WIDTH, 2 * D_FF), CONV_WIDTH ** -0.5),
        "b_conv": nrm(ks[19], (DEPTH, 2 * D_FF), 0.02),
        "w_down": nrm(ks[20], (DEPTH, D_FF, D_MODEL), D_FF ** -0.5),
    }


def reference(x, c, positions, w_ada, b_ada, g_norm1, w_in, b_in, g_q_lat, w_uq, g_kv_lat,
              w_ukv, g_q_head, g_k_head, w_branch_mla, w_branch_sb, w_out, g_norm2, w_up,
              w_conv, b_conv, w_down):
    b, s, _ = x.shape
    c_act = jax.nn.silu(c)
    for l in range(DEPTH):
        mod = c_act @ w_ada[l] + b_ada[l]
        sh1, sc1, gt1, sh2, sc2, gt2 = [m[:, None, :] for m in jnp.split(mod, N_MOD, axis=-1)]

        h = rms_norm(x, g_norm1[l]) * (1.0 + sc1) + sh1
        u = h @ w_in[l] + b_in[l]
        q_lat, kv_lat, k_rope, q_sb, k_sb, v_sb, gate_mla, gate_sb = jnp.split(u, IN_SPLIT_POINTS, axis=-1)

        q = (rms_norm(q_lat, g_q_lat[l]) @ w_uq[l]).reshape(b, s, MLA_HEADS, QK_HEAD_DIM)
        kv = (rms_norm(kv_lat, g_kv_lat[l]) @ w_ukv[l]).reshape(b, s, MLA_HEADS, QK_NOPE_DIM + V_HEAD_DIM)
        k_nope, v_mla = kv[..., :QK_NOPE_DIM], kv[..., QK_NOPE_DIM:]
        k_rope_h = jnp.broadcast_to(k_rope[:, :, None, :], (b, s, MLA_HEADS, QK_ROPE_DIM))
        k = jnp.concatenate([k_nope, k_rope_h], axis=-1)
        q = rms_norm(q, g_q_head[l])
        k = rms_norm(k, g_k_head[l])
        q = jnp.concatenate([q[..., :QK_NOPE_DIM], rope(q[..., QK_NOPE_DIM:], positions)], axis=-1)
        k = jnp.concatenate([k[..., :QK_NOPE_DIM], rope(k[..., QK_NOPE_DIM:], positions)], axis=-1)
        y_mla = causal_softmax_attention(q, k, v_mla)

        y_sb = stick_breaking_attention(
            q_sb.reshape(b, s, SB_HEADS, SB_HEAD_DIM),
            k_sb.reshape(b, s, SB_HEADS, SB_HEAD_DIM),
            v_sb.reshape(b, s, SB_HEADS, SB_HEAD_DIM))

        merged = (jax.nn.sigmoid(gate_mla) * (y_mla @ w_branch_mla[l])
                  + jax.nn.sigmoid(gate_sb) * (y_sb @ w_branch_sb[l]))
        x = x + gt1 * (merged @ w_out[l])

        h = rms_norm(x, g_norm2[l]) * (1.0 + sc2) + sh2
        up = causal_depthwise_conv(h @ w_up[l], w_conv[l], b_conv[l])
        a, v_ff = jnp.split(up, 2, axis=-1)
        x = x + gt2 * ((jax.nn.silu(a) * v_ff) @ w_down[l])
    return x
```

```python
import functools

import numpy as np
import jax
import jax.numpy as jnp
from jax import lax
from jax.experimental import pallas as pl
from jax.experimental.pallas import tpu as pltpu

F32 = jnp.float32
BF16 = jnp.bfloat16

MLA_HEADS = 4
QK_NOPE_DIM = 128
QK_ROPE_DIM = 64
QK_HEAD_DIM = QK_NOPE_DIM + QK_ROPE_DIM
V_HEAD_DIM = 128
Q_LORA_RANK = 256
KV_LORA_RANK = 128
ROPE_THETA = 10000.0
SB_HEADS = 4
SB_HEAD_DIM = 128
D_FF = 2816
CONV_WIDTH = 3
EPS = 1e-6
N_MOD = 6

LANES = 128
SUBLANES = 8
QK_PAD = 2 * LANES
VMEM_LIMIT = 56 * 1024 * 1024

EXP_ZERO_BELOW = -104.0
NEG_BIG = -1e30

TM_PROJ = 512
TM_POST = 512
TM_FFN = 512
FFN_CHUNK = 256
TQ_MLA = 512
TK_MLA = 512
TQ_SB = 256
TK_SB = 256
TS_ROPE = 2048


def _cparams(*sem):
    return pltpu.CompilerParams(dimension_semantics=sem, vmem_limit_bytes=VMEM_LIMIT)


def _const_spec(shape):
    nd = len(shape)
    return pl.BlockSpec(shape, lambda *_: (0,) * nd, pipeline_mode=pl.Buffered(1))


def _mod_kernel(c_ref, w_ref, b_ref, o_ref):
    c = c_ref[...]
    c_act = c * jax.nn.sigmoid(c)
    o_ref[0] = jnp.dot(c_act, w_ref[0], preferred_element_type=F32) + b_ref[0]


def _modulation(c, w_ada, b_ada):
    depth, d, nd = w_ada.shape
    b = c.shape[0]
    tn = 1024
    return pl.pallas_call(
        _mod_kernel,
        grid=(depth, nd // tn),
        in_specs=[
            pl.BlockSpec((b, d), lambda l, j: (0, 0)),
            pl.BlockSpec((1, d, tn), lambda l, j: (l, 0, j)),
            pl.BlockSpec((1, 1, tn), lambda l, j: (l, 0, j)),
        ],
        out_specs=pl.BlockSpec((1, b, tn), lambda l, j: (l, 0, j)),
        out_shape=jax.ShapeDtypeStruct((depth, b, nd), F32),
        compiler_params=_cparams("arbitrary", "arbitrary"),
        name="adaln_mod",
    )(c, w_ada, b_ada.reshape(depth, 1, nd))


def _rope_table_kernel(pos_ref, freq_ref, cos_ref, sin_ref):
    pos = pos_ref[0].astype(F32)
    ang = pos * freq_ref[...]
    lane = lax.broadcasted_iota(jnp.int32, ang.shape, 1)
    half = QK_ROPE_DIM // 2
    c = jnp.cos(ang)
    s = jnp.sin(ang)
    cos_ref[0] = jnp.where(lane < QK_ROPE_DIM, c, 0.0)
    sin_ref[0] = jnp.where(lane < half, -s, jnp.where(lane < QK_ROPE_DIM, s, 0.0))


def _rope_tables(positions):
    b, s = positions.shape
    ts = min(TS_ROPE, s)
    half = QK_ROPE_DIM // 2
    inv_freq = 1.0 / (ROPE_THETA ** (jnp.arange(half, dtype=F32) * (2.0 / QK_ROPE_DIM)))
    freq = jnp.tile(inv_freq, LANES // half).reshape(1, LANES)
    return pl.pallas_call(
        _rope_table_kernel,
        grid=(b, s // ts),
        in_specs=[
            pl.BlockSpec((1, ts, 1), lambda i, j: (i, j, 0)),
            pl.BlockSpec((1, LANES), lambda i, j: (0, 0)),
        ],
        out_specs=[
            pl.BlockSpec((1, ts, LANES), lambda i, j: (i, j, 0)),
            pl.BlockSpec((1, ts, LANES), lambda i, j: (i, j, 0)),
        ],
        out_shape=[jax.ShapeDtypeStruct((b, s, LANES), F32)] * 2,
        compiler_params=_cparams("arbitrary", "arbitrary"),
        name="rope_tables",
    )(positions.reshape(b, s, 1), freq)


C_QLAT = 0
C_KVLAT = C_QLAT + Q_LORA_RANK
C_KROPE = C_KVLAT + KV_LORA_RANK
C_SBQ = C_KROPE + LANES
C_SBK = C_SBQ + SB_HEADS * SB_HEAD_DIM
C_SBV = C_SBK + SB_HEADS * SB_HEAD_DIM
C_GMLA = C_SBV + SB_HEADS * SB_HEAD_DIM
D_IN_PAD_NO_GATE = C_GMLA


def _rms(v, n):
    return lax.rsqrt(jnp.sum(v * v, axis=-1, keepdims=True) * (1.0 / n) + EPS)


def _inproj_kernel(x_ref, sh_ref, sc_ref, g1_ref, win_ref, bin_ref, gql_ref, wuq_ref, gkl_ref,
                   wukv_ref, gq_ref, gk_ref, cos_ref, sin_ref,
                   qm_ref, km_ref, vm_ref, qs_ref, ks_ref, vs_ref, gm_ref, gs_ref, kn_ref):
    d = x_ref.shape[-1]
    x = x_ref[0]
    h = x * _rms(x, d) * g1_ref[...]
    h = h * (1.0 + sc_ref[0]) + sh_ref[0]
    hb = h.astype(BF16)

    def proj(lo, width):
        return (jnp.dot(hb, win_ref[:, lo:lo + width], preferred_element_type=F32)
                + bin_ref[:, lo:lo + width])

    cos_t = cos_ref[0]
    sin_t = sin_ref[0]

    def rope(v):
        return v * cos_t + pltpu.roll(v, LANES // 2, 1) * sin_t

    lat = proj(C_QLAT, C_SBQ)
    q_lat = lat[:, C_QLAT:C_KVLAT]
    kv_lat = lat[:, C_KVLAT:C_KROPE]
    k_rope = lat[:, C_KROPE:C_SBQ]

    qn = (q_lat * _rms(q_lat, Q_LORA_RANK) * gql_ref[...]).astype(BF16)
    q = jnp.dot(qn, wuq_ref[...], preferred_element_type=F32)
    gq = gq_ref[...]
    qk_scale = QK_HEAD_DIM ** -0.5
    for hd in range(MLA_HEADS):
        nope = q[:, hd * QK_PAD:hd * QK_PAD + LANES]
        rv = q[:, hd * QK_PAD + LANES:(hd + 1) * QK_PAD]
        ss = (jnp.sum(nope * nope, axis=-1, keepdims=True)
              + 0.5 * jnp.sum(rv * rv, axis=-1, keepdims=True))
        r = lax.rsqrt(ss * (1.0 / QK_HEAD_DIM) + EPS)
        qm_ref[0, hd, :, 0:LANES] = (nope * r * gq[:, 0:LANES] * qk_scale).astype(BF16)
        qm_ref[0, hd, :, LANES:QK_PAD] = (rope(rv * r * gq[:, LANES:QK_PAD]) * qk_scale).astype(BF16)

    kvn = (kv_lat * _rms(kv_lat, KV_LORA_RANK) * gkl_ref[...]).astype(BF16)
    kv = jnp.dot(kvn, wukv_ref[...], preferred_element_type=F32)
    gk = gk_ref[...]
    kr_rot = rope(k_rope * gk[:, LANES:QK_PAD])
    kr_ss = 0.5 * jnp.sum(k_rope * k_rope, axis=-1, keepdims=True)
    v_off = MLA_HEADS * QK_NOPE_DIM
    for hd in range(MLA_HEADS):
        kn = kv[:, hd * LANES:(hd + 1) * LANES]
        ss = jnp.sum(kn * kn, axis=-1, keepdims=True) + kr_ss
        r = lax.rsqrt(ss * (1.0 / QK_HEAD_DIM) + EPS)
        km_ref[0, hd, :, 0:LANES] = (kn * r * gk[:, 0:LANES]).astype(BF16)
        km_ref[0, hd, :, LANES:QK_PAD] = (kr_rot * r).astype(BF16)
        vm_ref[0, hd] = kv[:, v_off + hd * LANES:v_off + (hd + 1) * LANES].astype(BF16)

    sb_scale = SB_HEAD_DIM ** -0.5
    for hd in range(SB_HEADS):
        qs_ref[0, hd] = (proj(C_SBQ + hd * LANES, LANES) * sb_scale).astype(BF16)
        kb = proj(C_SBK + hd * LANES, LANES).astype(BF16)
        ks_ref[0, hd] = kb
        kf = kb.astype(F32)
        n2 = jnp.max(jnp.sum(kf * kf, axis=-1, keepdims=True), axis=0, keepdims=True)
        kn_ref[0, hd, 0] = jnp.broadcast_to(n2, (SUBLANES, LANES))
        vs_ref[0, hd] = proj(C_SBV + hd * LANES, LANES).astype(BF16)

    gm_ref[0] = jax.nn.sigmoid(proj(C_GMLA, d)).astype(BF16)
    gs_ref[0] = jax.nn.sigmoid(proj(C_GMLA + d, d)).astype(BF16)


def _in_projection(x, sh1, sc1, g1, w_in, b_in, g_q_lat, w_uq, g_kv_lat, w_ukv, gq, gk, cos_t, sin_t):
    b, s, d = x.shape
    tm = min(TM_PROJ, s)
    nt = s // tm
    d_in = w_in.shape[1]
    tok = lambda w: pl.BlockSpec((1, tm, w), lambda i, j: (i, j, 0))
    per_batch = pl.BlockSpec((1, 1, d), lambda i, j: (i, 0, 0))
    head = lambda h, w: pl.BlockSpec((1, h, tm, w), lambda i, j: (i, 0, j, 0))
    out_shape = [
        jax.ShapeDtypeStruct((b, MLA_HEADS, s, QK_PAD), BF16),
        jax.ShapeDtypeStruct((b, MLA_HEADS, s, QK_PAD), BF16),
        jax.ShapeDtypeStruct((b, MLA_HEADS, s, V_HEAD_DIM), BF16),
        jax.ShapeDtypeStruct((b, SB_HEADS, s, SB_HEAD_DIM), BF16),
        jax.ShapeDtypeStruct((b, SB_HEADS, s, SB_HEAD_DIM), BF16),
        jax.ShapeDtypeStruct((b, SB_HEADS, s, SB_HEAD_DIM), BF16),
        jax.ShapeDtypeStruct((b, s, d), BF16),
        jax.ShapeDtypeStruct((b, s, d), BF16),
        jax.ShapeDtypeStruct((b, SB_HEADS, nt, SUBLANES, LANES), F32),
    ]
    out_specs = [
        head(MLA_HEADS, QK_PAD), head(MLA_HEADS, QK_PAD), head(MLA_HEADS, V_HEAD_DIM),
        head(SB_HEADS, SB_HEAD_DIM), head(SB_HEADS, SB_HEAD_DIM), head(SB_HEADS, SB_HEAD_DIM),
        tok(d), tok(d),
        pl.BlockSpec((1, SB_HEADS, 1, SUBLANES, LANES), lambda i, j: (i, 0, j, 0, 0)),
    ]
    in_specs = [
        tok(d), per_batch, per_batch, _const_spec((1, d)),
        _const_spec((d, d_in)), _const_spec((1, d_in)),
        _const_spec((1, Q_LORA_RANK)), _const_spec(w_uq.shape),
        _const_spec((1, KV_LORA_RANK)), _const_spec(w_ukv.shape),
        _const_spec((1, QK_PAD)), _const_spec((1, QK_PAD)),
        tok(LANES), tok(LANES),
    ]
    return pl.pallas_call(
        _inproj_kernel,
        grid=(b, nt),
        in_specs=in_specs,
        out_specs=out_specs,
        out_shape=out_shape,
        compiler_params=_cparams("arbitrary", "arbitrary"),
        name="in_projection",
    )(x, sh1, sc1, g1, w_in, b_in, g_q_lat, w_uq, g_kv_lat, w_ukv, gq, gk, cos_t, sin_t)


def _mla_kernel(q_ref, k_ref, v_ref, o_ref, *, tq, tk):
    qi = pl.program_id(2)
    q = q_ref[0, 0]
    dv = v_ref.shape[-1]

    def step(kc, carry, masked):
        m, l, acc = carry
        start = pl.multiple_of(kc * tk, tk)
        k = k_ref[0, 0, pl.ds(start, tk), :]
        v = v_ref[0, 0, pl.ds(start, tk), :]
        s = lax.dot_general(q, k, (((1,), (1,)), ((), ())), preferred_element_type=F32)
        if masked:
            row = qi * tq + lax.broadcasted_iota(jnp.int32, (tq, tk), 0)
            col = start + lax.broadcasted_iota(jnp.int32, (tq, tk), 1)
            s = jnp.where(col <= row, s, NEG_BIG)
        m_new = jnp.maximum(m, jnp.max(s, axis=1, keepdims=True))
        alpha = jnp.exp(m - m_new)
        p = jnp.exp(s - m_new)
        l = alpha * l + jnp.sum(p, axis=1, keepdims=True)
        acc = alpha * acc + jnp.dot(p.astype(BF16), v, preferred_element_type=F32)
        return m_new, l, acc

    init = (jnp.full((tq, 1), NEG_BIG, F32), jnp.zeros((tq, 1), F32), jnp.zeros((tq, dv), F32))
    per_q = tq // tk
    n_full = qi * per_q
    carry = lax.fori_loop(0, n_full, lambda kc, c: step(kc, c, False), init)
    for j in range(per_q):
        carry = step(n_full + j, carry, True)
    _, l, acc = carry
    o_ref[0] = (acc / l).astype(o_ref.dtype)


def _mla_attention(q, k, v):
    b, h, s, _ = q.shape
    dv = v.shape[-1]
    tq = min(TQ_MLA, s)
    tk = min(TK_MLA, tq)
    return pl.pallas_call(
        functools.partial(_mla_kernel, tq=tq, tk=tk),
        grid=(b, h, s // tq),
        in_specs=[
            pl.BlockSpec((1, 1, tq, QK_PAD), lambda i, j, t: (i, j, t, 0)),
            pl.BlockSpec((1, 1, s, QK_PAD), lambda i, j, t: (i, j, 0, 0)),
            pl.BlockSpec((1, 1, s, dv), lambda i, j, t: (i, j, 0, 0)),
        ],
        out_specs=pl.BlockSpec((1, tq, dv), lambda i, j, t: (i, t, j)),
        out_shape=jax.ShapeDtypeStruct((b, s, h * dv), BF16),
        compiler_params=_cparams("arbitrary", "arbitrary", "arbitrary"),
        name="mla_attention",
    )(q, k, v)


def _sb_kernel(q_ref, k_ref, v_ref, kn_ref, o_ref, *, tq, tk):
    qi = pl.program_id(2)
    q = q_ref[0, 0]
    d = q.shape[-1]
    qf = q.astype(F32)
    q_norm = jnp.sqrt(jnp.sum(qf * qf, axis=-1, keepdims=True))
    k_norm = jnp.sqrt(jnp.max(kn_ref[0, 0]))
    z_bound = q_norm * (k_norm * 1.001) + 1e-3

    tri = (lax.broadcasted_iota(jnp.int32, (tk, tk), 0)
           >= lax.broadcasted_iota(jnp.int32, (tk, tk), 1)).astype(BF16)
    row = qi * tq + lax.broadcasted_iota(jnp.int32, (tq, tk), 0)
    col0 = lax.broadcasted_iota(jnp.int32, (tq, tk), 1)

    def cond(state):
        j, alive, _, _ = state
        return jnp.logical_and(j >= 0, alive)

    def body(state):
        j, _, carry, acc = state
        start = pl.multiple_of(j * tk, tk)
        k = k_ref[0, 0, pl.ds(start, tk), :]
        v = v_ref[0, 0, pl.ds(start, tk), :]
        z = lax.dot_general(q, k, (((1,), (1,)), ((), ())), preferred_element_type=F32)
        past = (start + col0) < row
        ls = jnp.minimum(-z, 0.0) - jnp.log1p(jnp.exp(-jnp.abs(z)))
        ls = jnp.where(past, ls, 0.0)
        ls_hi = ls.astype(BF16)
        ls_lo = (ls - ls_hi.astype(F32)).astype(BF16)
        rc = (jnp.dot(ls_hi, tri, preferred_element_type=F32)
              + jnp.dot(ls_lo, tri, preferred_element_type=F32))
        a = jnp.where(past, jnp.exp(z + rc + carry), 0.0)
        acc = acc + jnp.dot(a.astype(BF16), v, preferred_element_type=F32)
        carry = carry + jnp.sum(ls, axis=1, keepdims=True)
        alive = jnp.max(z_bound + carry) > EXP_ZERO_BELOW
        return j - 1, alive, carry, acc

    n_chunks = (qi + 1) * (tq // tk)
    state = (n_chunks - 1, jnp.bool_(True), jnp.zeros((tq, 1), F32), jnp.zeros((tq, d), F32))
    _, _, _, acc = lax.while_loop(cond, body, state)
    o_ref[0] = acc.astype(o_ref.dtype)


def _sb_attention(q, k, v, kn):
    b, h, s, d = q.shape
    nt = kn.shape[2]
    tq = min(TQ_SB, s)
    tk = min(TK_SB, tq)
    return pl.pallas_call(
        functools.partial(_sb_kernel, tq=tq, tk=tk),
        grid=(b, h, s // tq),
        in_specs=[
            pl.BlockSpec((1, 1, tq, d), lambda i, j, t: (i, j, t, 0)),
            pl.BlockSpec((1, 1, s, d), lambda i, j, t: (i, j, 0, 0)),
            pl.BlockSpec((1, 1, s, d), lambda i, j, t: (i, j, 0, 0)),
            pl.BlockSpec((1, 1, nt, SUBLANES, LANES), lambda i, j, t: (i, j, 0, 0, 0)),
        ],
        out_specs=pl.BlockSpec((1, tq, d), lambda i, j, t: (i, t, j)),
        out_shape=jax.ShapeDtypeStruct((b, s, h * d), BF16),
        compiler_params=_cparams("arbitrary", "arbitrary", "arbitrary"),
        name="sb_attention",
    )(q, k, v, kn)


def _post_kernel(ym_ref, ys_ref, gm_ref, gs_ref, x_ref, gt_ref, sh_ref, sc_ref, g2_ref,
                 wbm_ref, wbs_ref, wo_ref, x1_ref, h2_ref):
    d = x_ref.shape[-1]
    bm = jnp.dot(ym_ref[0], wbm_ref[...], preferred_element_type=F32)
    bs = jnp.dot(ys_ref[0], wbs_ref[...], preferred_element_type=F32)
    merged = gm_ref[0].astype(F32) * bm + gs_ref[0].astype(F32) * bs
    y = jnp.dot(merged.astype(BF16), wo_ref[...], preferred_element_type=F32)
    x1 = x_ref[0] + gt_ref[0] * y
    x1_ref[0] = x1
    h2 = x1 * _rms(x1, d) * g2_ref[...]
    h2_ref[0] = (h2 * (1.0 + sc_ref[0]) + sh_ref[0]).astype(BF16)


def _post_attention(y_mla, y_sb, gm, gs, x, gt1, sh2, sc2, g2, w_bm, w_bs, w_out):
    b, s, d = x.shape
    tm = min(TM_POST, s)
    tok = lambda w: pl.BlockSpec((1, tm, w), lambda i, j: (i, j, 0))
    per_batch = pl.BlockSpec((1, 1, d), lambda i, j: (i, 0, 0))
    return pl.pallas_call(
        _post_kernel,
        grid=(b, s // tm),
        in_specs=[
            tok(y_mla.shape[-1]), tok(y_sb.shape[-1]), tok(d), tok(d), tok(d),
            per_batch, per_batch, per_batch, _const_spec((1, d)),
            _const_spec(w_bm.shape), _const_spec(w_bs.shape), _const_spec(w_out.shape),
        ],
        out_specs=[tok(d), tok(d)],
        out_shape=[jax.ShapeDtypeStruct((b, s, d), F32), jax.ShapeDtypeStruct((b, s, d), BF16)],
        compiler_params=_cparams("arbitrary", "arbitrary"),
        name="post_attention",
    )(y_mla, y_sb, gm, gs, x, gt1, sh2, sc2, g2, w_bm, w_bs, w_out)


def _ffn_kernel(h_ref, x_ref, gt_ref, wup_ref, wcv_ref, wdn_ref, o_ref, carry_ref, buf_ref, *, tm, fc):
    n_chunks = wdn_ref.shape[0]

    @pl.when(pl.program_id(1) == 0)
    def _():
        carry_ref[...] = jnp.zeros(carry_ref.shape, F32)

    hb = h_ref[0]

    def conv_up(idx):
        u = jnp.dot(hb, wup_ref[idx], preferred_element_type=F32)
        buf_ref[0:SUBLANES, :] = carry_ref[idx]
        buf_ref[SUBLANES:SUBLANES + tm, :] = u
        carry_ref[idx] = u[tm - SUBLANES:tm, :]
        u1 = buf_ref[SUBLANES - 1:SUBLANES - 1 + tm, :]
        u2 = buf_ref[SUBLANES - 2:SUBLANES - 2 + tm, :]
        w = wcv_ref[idx]
        return w[0:1] * u2 + w[1:2] * u1 + w[2:3] * u + w[3:4]

    acc = jnp.zeros((tm, o_ref.shape[-1]), F32)
    for c in range(n_chunks):
        a = conv_up(c)
        v = conv_up(n_chunks + c)
        g = (a * jax.nn.sigmoid(a) * v).astype(BF16)
        acc = acc + jnp.dot(g, wdn_ref[c], preferred_element_type=F32)
    o_ref[0] = x_ref[0] + gt_ref[0] * acc


def _conv_ffn(h2, x1, gt2, w_up, w_cv, w_dn):
    b, s, d = x1.shape
    tm = min(TM_FFN, s)
    n2, _, fc = w_up.shape
    tok = lambda w: pl.BlockSpec((1, tm, w), lambda i, j: (i, j, 0))
    per_batch = pl.BlockSpec((1, 1, d), lambda i, j: (i, 0, 0))
    return pl.pallas_call(
        functools.partial(_ffn_kernel, tm=tm, fc=fc),
        grid=(b, s // tm),
        in_specs=[tok(d), tok(d), per_batch, _const_spec(w_up.shape), _const_spec(w_cv.shape),
                  _const_spec(w_dn.shape)],
        out_specs=tok(d),
        out_shape=jax.ShapeDtypeStruct((b, s, d), F32),
        scratch_shapes=[pltpu.VMEM((n2, SUBLANES, fc), F32), pltpu.VMEM((tm + SUBLANES, fc), F32)],
        compiler_params=_cparams("arbitrary", "arbitrary"),
        name="conv_ffn",
    )(h2, x1, gt2, w_up, w_cv, w_dn)


def _rope_dup_index():
    half = QK_ROPE_DIM // 2
    x1 = np.arange(half)
    x2 = half + np.arange(half)
    return np.concatenate([x1, x2, x2, x1])


def _prep_weights(w_in, b_in, w_uq, w_ukv, g_q_head, g_k_head, w_up, w_conv, b_conv, w_down):
    depth = w_in.shape[0]
    dup = _rope_dup_index()
    rope_lo = Q_LORA_RANK + KV_LORA_RANK
    perm_in = np.concatenate([np.arange(rope_lo), rope_lo + dup,
                              np.arange(rope_lo + QK_ROPE_DIM, w_in.shape[-1])])
    head_idx = np.concatenate([np.arange(QK_NOPE_DIM), QK_NOPE_DIM + dup])
    perm_uq = np.concatenate([h * QK_HEAD_DIM + head_idx for h in range(MLA_HEADS)])
    kv_w = QK_NOPE_DIM + V_HEAD_DIM
    perm_ukv = np.concatenate(
        [h * kv_w + np.arange(QK_NOPE_DIM) for h in range(MLA_HEADS)]
        + [h * kv_w + QK_NOPE_DIM + np.arange(V_HEAD_DIM) for h in range(MLA_HEADS)])
    n_chunks = D_FF // FFN_CHUNK
    d = w_up.shape[1]
    w_up_c = w_up.astype(BF16).reshape(depth, d, 2 * n_chunks, FFN_CHUNK).transpose(0, 2, 1, 3)
    cv = jnp.concatenate([w_conv, b_conv[:, None, :],
                          jnp.zeros((depth, SUBLANES - CONV_WIDTH - 1, 2 * D_FF), F32)], axis=1)
    cv = cv.reshape(depth, SUBLANES, 2 * n_chunks, FFN_CHUNK).transpose(0, 2, 1, 3)
    return dict(
        w_in=w_in[:, :, perm_in].astype(BF16),
        b_in=b_in[:, perm_in].reshape(depth, 1, -1),
        w_uq=w_uq[:, :, perm_uq].astype(BF16),
        w_ukv=w_ukv[:, :, perm_ukv].astype(BF16),
        gq=g_q_head[:, head_idx].reshape(depth, 1, -1),
        gk=g_k_head[:, head_idx].reshape(depth, 1, -1),
        w_up=w_up_c,
        w_cv=cv,
        w_dn=w_down.astype(BF16).reshape(depth, n_chunks, FFN_CHUNK, -1),
    )


def kernel(x, c, positions, w_ada, b_ada, g_norm1, w_in, b_in, g_q_lat, w_uq, g_kv_lat, w_ukv,
           g_q_head, g_k_head, w_branch_mla, w_branch_sb, w_out, g_norm2, w_up, w_conv, b_conv,
           w_down):
    depth = w_in.shape[0]
    b, s, d = x.shape
    mod = _modulation(c, w_ada, b_ada)
    cos_t, sin_t = _rope_tables(positions)
    pw = _prep_weights(w_in, b_in, w_uq, w_ukv, g_q_head, g_k_head, w_up, w_conv, b_conv, w_down)
    w_bm = w_branch_mla.astype(BF16)
    w_bs = w_branch_sb.astype(BF16)
    w_o = w_out.astype(BF16)
    row = lambda g: g.reshape(1, -1)

    for l in range(depth):
        sh1, sc1, gt1, sh2, sc2, gt2 = [mod[l, :, i * d:(i + 1) * d].reshape(b, 1, d)
                                        for i in range(N_MOD)]
        qm, km, vm, qs, ks, vs, gm, gs, kn = _in_projection(
            x, sh1, sc1, row(g_norm1[l]), pw["w_in"][l], pw["b_in"][l], row(g_q_lat[l]),
            pw["w_uq"][l], row(g_kv_lat[l]), pw["w_ukv"][l], pw["gq"][l], pw["gk"][l], cos_t, sin_t)
        y_mla = _mla_attention(qm, km, vm)
        y_sb = _sb_attention(qs, ks, vs, kn)
        x1, h2 = _post_attention(y_mla, y_sb, gm, gs, x, gt1, sh2, sc2, row(g_norm2[l]),
                                 w_bm[l], w_bs[l], w_o[l])
        x = _conv_ffn(h2, x1, gt2, pw["w_up"][l], pw["w_cv"][l], pw["w_dn"][l])
    return x
```

```python
import functools

import numpy as np
import jax
import jax.numpy as jnp
from jax import lax
from jax.experimental import pallas as pl
from jax.experimental.pallas import tpu as pltpu

F32 = jnp.float32
BF16 = jnp.bfloat16

MLA_HEADS = 4
QK_NOPE_DIM = 128
QK_ROPE_DIM = 64
QK_HEAD_DIM = QK_NOPE_DIM + QK_ROPE_DIM
V_HEAD_DIM = 128
Q_LORA_RANK = 256
KV_LORA_RANK = 128
ROPE_THETA = 10000.0
SB_HEADS = 4
SB_HEAD_DIM = 128
D_FF = 2816
CONV_WIDTH = 3
EPS = 1e-6
N_MOD = 6

LANES = 128
SUBLANES = 8
QK_PAD = 2 * LANES
VMEM_LIMIT = 56 * 1024 * 1024

EXP_ZERO_BELOW = -104.0
NEG_BIG = -1e30
LOG2_E = 1.4426950408889634

TM_PROJ = 512
TM_POST = 512
TM_FFN = 512
FFN_CHUNK = 256
TQ_MLA = 1024
TK_MLA = 256
TQ_SB = 256
TS_ROPE = 2048


def _cparams(*sem):
    return pltpu.CompilerParams(dimension_semantics=sem, vmem_limit_bytes=VMEM_LIMIT)


def _const_spec(shape):
    nd = len(shape)
    return pl.BlockSpec(shape, lambda *_: (0,) * nd, pipeline_mode=pl.Buffered(1))


def _mod_kernel(c_ref, w_ref, b_ref, o_ref):
    c = c_ref[...]
    c_act = c * jax.nn.sigmoid(c)
    o_ref[0] = jnp.dot(c_act, w_ref[0], preferred_element_type=F32) + b_ref[0]


def _modulation(c, w_ada, b_ada):
    depth, d, nd = w_ada.shape
    b = c.shape[0]
    tn = 1024
    return pl.pallas_call(
        _mod_kernel,
        grid=(depth, nd // tn),
        in_specs=[
            pl.BlockSpec((b, d), lambda l, j: (0, 0)),
            pl.BlockSpec((1, d, tn), lambda l, j: (l, 0, j)),
            pl.BlockSpec((1, 1, tn), lambda l, j: (l, 0, j)),
        ],
        out_specs=pl.BlockSpec((1, b, tn), lambda l, j: (l, 0, j)),
        out_shape=jax.ShapeDtypeStruct((depth, b, nd), F32),
        compiler_params=_cparams("arbitrary", "arbitrary"),
        name="adaln_mod",
    )(c, w_ada, b_ada.reshape(depth, 1, nd))


def _rope_table_kernel(pos_ref, freq_ref, cos_ref, sin_ref):
    pos = pos_ref[0].astype(F32)
    ang = pos * freq_ref[...]
    lane = lax.broadcasted_iota(jnp.int32, ang.shape, 1)
    half = QK_ROPE_DIM // 2
    c = jnp.cos(ang)
    s = jnp.sin(ang)
    cos_ref[0] = jnp.where(lane < QK_ROPE_DIM, c, 0.0)
    sin_ref[0] = jnp.where(lane < half, -s, jnp.where(lane < QK_ROPE_DIM, s, 0.0))


def _rope_tables(positions):
    b, s = positions.shape
    ts = min(TS_ROPE, s)
    half = QK_ROPE_DIM // 2
    inv_freq = 1.0 / (ROPE_THETA ** (jnp.arange(half, dtype=F32) * (2.0 / QK_ROPE_DIM)))
    freq = jnp.tile(inv_freq, LANES // half).reshape(1, LANES)
    return pl.pallas_call(
        _rope_table_kernel,
        grid=(b, s // ts),
        in_specs=[
            pl.BlockSpec((1, ts, 1), lambda i, j: (i, j, 0)),
            pl.BlockSpec((1, LANES), lambda i, j: (0, 0)),
        ],
        out_specs=[
            pl.BlockSpec((1, ts, LANES), lambda i, j: (i, j, 0)),
            pl.BlockSpec((1, ts, LANES), lambda i, j: (i, j, 0)),
        ],
        out_shape=[jax.ShapeDtypeStruct((b, s, LANES), F32)] * 2,
        compiler_params=_cparams("arbitrary", "arbitrary"),
        name="rope_tables",
    )(positions.reshape(b, s, 1), freq)


C_QLAT = 0
C_KVLAT = C_QLAT + Q_LORA_RANK
C_KROPE = C_KVLAT + KV_LORA_RANK
C_SBQ = C_KROPE + LANES
C_SBK = C_SBQ + SB_HEADS * SB_HEAD_DIM
C_SBV = C_SBK + SB_HEADS * SB_HEAD_DIM
C_GMLA = C_SBV + SB_HEADS * SB_HEAD_DIM
D_IN_PAD_NO_GATE = C_GMLA


def _rms(v, n):
    return lax.rsqrt(jnp.sum(v * v, axis=-1, keepdims=True) * (1.0 / n) + EPS)


def _inproj_kernel(x_ref, sh_ref, sc_ref, g1_ref, win_ref, bin_ref, gql_ref, wuq_ref, gkl_ref,
                   wukv_ref, gq_ref, gk_ref, cos_ref, sin_ref,
                   qm_ref, km_ref, vm_ref, qs_ref, ks_ref, vs_ref, gm_ref, gs_ref, kn_ref):
    d = x_ref.shape[-1]
    x = x_ref[0]
    h = x * _rms(x, d) * g1_ref[...]
    h = h * (1.0 + sc_ref[0]) + sh_ref[0]
    hb = h.astype(BF16)

    def proj(lo, width):
        return (jnp.dot(hb, win_ref[:, lo:lo + width], preferred_element_type=F32)
                + bin_ref[:, lo:lo + width])

    cos_t = cos_ref[0]
    sin_t = sin_ref[0]

    def rope(v):
        return v * cos_t + pltpu.roll(v, LANES // 2, 1) * sin_t

    lat = proj(C_QLAT, C_SBQ)
    q_lat = lat[:, C_QLAT:C_KVLAT]
    kv_lat = lat[:, C_KVLAT:C_KROPE]
    k_rope = lat[:, C_KROPE:C_SBQ]

    qn = (q_lat * _rms(q_lat, Q_LORA_RANK) * gql_ref[...]).astype(BF16)
    q = jnp.dot(qn, wuq_ref[...], preferred_element_type=F32)
    gq = gq_ref[...]
    qk_scale = QK_HEAD_DIM ** -0.5 * LOG2_E
    for hd in range(MLA_HEADS):
        nope = q[:, hd * QK_PAD:hd * QK_PAD + LANES]
        rv = q[:, hd * QK_PAD + LANES:(hd + 1) * QK_PAD]
        ss = (jnp.sum(nope * nope, axis=-1, keepdims=True)
              + 0.5 * jnp.sum(rv * rv, axis=-1, keepdims=True))
        r = lax.rsqrt(ss * (1.0 / QK_HEAD_DIM) + EPS)
        qm_ref[0, hd, :, 0:LANES] = (nope * r * gq[:, 0:LANES] * qk_scale).astype(BF16)
        qm_ref[0, hd, :, LANES:QK_PAD] = (rope(rv * r * gq[:, LANES:QK_PAD]) * qk_scale).astype(BF16)

    kvn = (kv_lat * _rms(kv_lat, KV_LORA_RANK) * gkl_ref[...]).astype(BF16)
    kv = jnp.dot(kvn, wukv_ref[...], preferred_element_type=F32)
    gk = gk_ref[...]
    kr_rot = rope(k_rope * gk[:, LANES:QK_PAD])
    kr_ss = 0.5 * jnp.sum(k_rope * k_rope, axis=-1, keepdims=True)
    v_off = MLA_HEADS * QK_NOPE_DIM
    for hd in range(MLA_HEADS):
        kn = kv[:, hd * LANES:(hd + 1) * LANES]
        ss = jnp.sum(kn * kn, axis=-1, keepdims=True) + kr_ss
        r = lax.rsqrt(ss * (1.0 / QK_HEAD_DIM) + EPS)
        km_ref[0, hd, :, 0:LANES] = (kn * r * gk[:, 0:LANES]).astype(BF16)
        km_ref[0, hd, :, LANES:QK_PAD] = (kr_rot * r).astype(BF16)
        vm_ref[0, hd] = kv[:, v_off + hd * LANES:v_off + (hd + 1) * LANES].astype(BF16)

    sb_scale = SB_HEAD_DIM ** -0.5
    for hd in range(SB_HEADS):
        qs_ref[0, hd] = (proj(C_SBQ + hd * LANES, LANES) * sb_scale).astype(BF16)
        kb = proj(C_SBK + hd * LANES, LANES).astype(BF16)
        ks_ref[0, hd] = kb
        kf = kb.astype(F32)
        n2 = jnp.max(jnp.sum(kf * kf, axis=-1, keepdims=True), axis=0, keepdims=True)
        kn_ref[0, hd, 0] = jnp.broadcast_to(n2, (SUBLANES, LANES))
        vs_ref[0, hd] = proj(C_SBV + hd * LANES, LANES).astype(BF16)

    gm_ref[0] = jax.nn.sigmoid(proj(C_GMLA, d)).astype(BF16)
    gs_ref[0] = jax.nn.sigmoid(proj(C_GMLA + d, d)).astype(BF16)


def _in_projection(x, sh1, sc1, g1, w_in, b_in, g_q_lat, w_uq, g_kv_lat, w_ukv, gq, gk, cos_t, sin_t):
    b, s, d = x.shape
    tm = min(TM_PROJ, s)
    nt = s // tm
    d_in = w_in.shape[1]
    tok = lambda w: pl.BlockSpec((1, tm, w), lambda i, j: (i, j, 0))
    per_batch = pl.BlockSpec((1, 1, d), lambda i, j: (i, 0, 0))
    head = lambda h, w: pl.BlockSpec((1, h, tm, w), lambda i, j: (i, 0, j, 0))
    out_shape = [
        jax.ShapeDtypeStruct((b, MLA_HEADS, s, QK_PAD), BF16),
        jax.ShapeDtypeStruct((b, MLA_HEADS, s, QK_PAD), BF16),
        jax.ShapeDtypeStruct((b, MLA_HEADS, s, V_HEAD_DIM), BF16),
        jax.ShapeDtypeStruct((b, SB_HEADS, s, SB_HEAD_DIM), BF16),
        jax.ShapeDtypeStruct((b, SB_HEADS, s, SB_HEAD_DIM), BF16),
        jax.ShapeDtypeStruct((b, SB_HEADS, s, SB_HEAD_DIM), BF16),
        jax.ShapeDtypeStruct((b, s, d), BF16),
        jax.ShapeDtypeStruct((b, s, d), BF16),
        jax.ShapeDtypeStruct((b, SB_HEADS, nt, SUBLANES, LANES), F32),
    ]
    out_specs = [
        head(MLA_HEADS, QK_PAD), head(MLA_HEADS, QK_PAD), head(MLA_HEADS, V_HEAD_DIM),
        head(SB_HEADS, SB_HEAD_DIM), head(SB_HEADS, SB_HEAD_DIM), head(SB_HEADS, SB_HEAD_DIM),
        tok(d), tok(d),
        pl.BlockSpec((1, SB_HEADS, 1, SUBLANES, LANES), lambda i, j: (i, 0, j, 0, 0)),
    ]
    in_specs = [
        tok(d), per_batch, per_batch, _const_spec((1, d)),
        _const_spec((d, d_in)), _const_spec((1, d_in)),
        _const_spec((1, Q_LORA_RANK)), _const_spec(w_uq.shape),
        _const_spec((1, KV_LORA_RANK)), _const_spec(w_ukv.shape),
        _const_spec((1, QK_PAD)), _const_spec((1, QK_PAD)),
        tok(LANES), tok(LANES),
    ]
    return pl.pallas_call(
        _inproj_kernel,
        grid=(b, nt),
        in_specs=in_specs,
        out_specs=out_specs,
        out_shape=out_shape,
        compiler_params=_cparams("arbitrary", "arbitrary"),
        name="in_projection",
    )(x, sh1, sc1, g1, w_in, b_in, g_q_lat, w_uq, g_kv_lat, w_ukv, gq, gk, cos_t, sin_t)


def _mla_kernel(q_ref, k_ref, v_ref, o_ref, m_ref, l_ref, acc_ref, s_ref, *, tq, tk):
    qi = pl.program_id(2)
    per_q = tq // tk
    n_full = qi * per_q

    def scores(kc, r0):
        start = pl.multiple_of(kc * tk, tk)
        k = k_ref[0, 0, pl.ds(start, tk), :]
        return lax.dot_general(q_ref[0, 0, r0:tq, :], k, (((1,), (1,)), ((), ())),
                               preferred_element_type=F32)

    def update(kc, s, r0, masked):
        start = pl.multiple_of(kc * tk, tk)
        v = v_ref[0, 0, pl.ds(start, tk), :]
        if masked:
            row = r0 + lax.broadcasted_iota(jnp.int32, s.shape, 0)
            col = (start - qi * tq) + lax.broadcasted_iota(jnp.int32, s.shape, 1)
            s = jnp.where(col <= row, s, NEG_BIG)
        m = m_ref[r0:tq, :]
        m_new = jnp.maximum(m, jnp.max(s, axis=1, keepdims=True))
        alpha = jnp.exp2(m - m_new)
        p = jnp.exp2(s - pltpu.repeat(m_new, tk // LANES, axis=1))
        m_ref[r0:tq, :] = m_new
        l_ref[r0:tq, :] = alpha * l_ref[r0:tq, :] + (p[:, :LANES] + p[:, LANES:])
        acc_ref[r0:tq, :] = alpha * acc_ref[r0:tq, :] + jnp.dot(
            p.astype(BF16), v, preferred_element_type=F32)

    m_ref[...] = jnp.full(m_ref.shape, NEG_BIG, F32)
    l_ref[...] = jnp.zeros(l_ref.shape, F32)
    acc_ref[...] = jnp.zeros(acc_ref.shape, F32)
    s_ref[...] = scores(0, 0)

    def body(t, _):
        c = per_q * t
        s_cur = s_ref[...]
        for u in range(per_q):
            s_nxt = scores(c + u + 1, 0)
            update(c + u, s_cur, 0, False)
            s_cur = s_nxt
        s_ref[...] = s_cur
        return 0

    lax.fori_loop(0, qi, body, 0)
    update(n_full, s_ref[...], 0, True)
    for j in range(1, per_q):
        update(n_full + j, scores(n_full + j, j * tk), j * tk, True)
    l = jnp.sum(l_ref[...], axis=1, keepdims=True)
    o_ref[0] = (acc_ref[...] / l).astype(o_ref.dtype)


def _mla_attention(q, k, v):
    b, h, s, _ = q.shape
    dv = v.shape[-1]
    tq = min(TQ_MLA, s)
    tk = min(TK_MLA, tq)
    return pl.pallas_call(
        functools.partial(_mla_kernel, tq=tq, tk=tk),
        grid=(b, h, s // tq),
        in_specs=[
            pl.BlockSpec((1, 1, tq, QK_PAD), lambda i, j, t: (i, j, t, 0)),
            pl.BlockSpec((1, 1, s, QK_PAD), lambda i, j, t: (i, j, 0, 0)),
            pl.BlockSpec((1, 1, s, dv), lambda i, j, t: (i, j, 0, 0)),
        ],
        out_specs=pl.BlockSpec((1, tq, dv), lambda i, j, t: (i, t, j)),
        out_shape=jax.ShapeDtypeStruct((b, s, h * dv), BF16),
        scratch_shapes=[pltpu.VMEM((tq, LANES), F32), pltpu.VMEM((tq, LANES), F32),
                        pltpu.VMEM((tq, dv), F32), pltpu.VMEM((tq, tk), F32)],
        compiler_params=_cparams("arbitrary", "arbitrary", "arbitrary"),
        name="mla_attention",
    )(q, k, v)


def _sb_kernel(q_ref, k_ref, v_ref, kn_ref, o_ref, *, tq):
    qi = pl.program_id(1)
    nh, d = q_ref.shape[1], q_ref.shape[-1]
    heads = range(nh)
    qs = [q_ref[0, h] for h in heads]
    z_bound = []
    for h in heads:
        qf = qs[h].astype(F32)
        q_norm = jnp.sqrt(jnp.sum(qf * qf, axis=-1, keepdims=True))
        k_norm = jnp.sqrt(jnp.max(kn_ref[0, h]))
        z_bound.append(q_norm * (k_norm * 1.001) + 1e-3)

    tri = (lax.broadcasted_iota(jnp.int32, (tq, tq), 0)
           >= lax.broadcasted_iota(jnp.int32, (tq, tq), 1)).astype(BF16)

    def chunk(h, j, carry, acc, masked):
        start = pl.multiple_of(j * tq, tq)
        k = k_ref[0, h, pl.ds(start, tq), :]
        v = v_ref[0, h, pl.ds(start, tq), :]
        z = lax.dot_general(qs[h], k, (((1,), (1,)), ((), ())), preferred_element_type=F32)
        ls = jnp.minimum(-z, 0.0) - jnp.log(1.0 + jnp.exp(-jnp.abs(z)))
        if masked:
            past = (lax.broadcasted_iota(jnp.int32, (tq, tq), 1)
                    < lax.broadcasted_iota(jnp.int32, (tq, tq), 0))
            ls = jnp.where(past, ls, 0.0)
        ls_hi = ls.astype(BF16)
        ls_lo = (ls - ls_hi.astype(F32)).astype(BF16)
        rc = (jnp.dot(ls_hi, tri, preferred_element_type=F32)
              + jnp.dot(ls_lo, tri, preferred_element_type=F32))
        a = jnp.exp(z + rc + carry)
        if masked:
            a = jnp.where(past, a, 0.0)
        acc = acc + jnp.dot(a.astype(BF16), v, preferred_element_type=F32)
        carry = carry + jnp.sum(ls, axis=1, keepdims=True)
        return carry, acc

    def any_alive(carries):
        worst = z_bound[0] + carries[0]
        for h in heads[1:]:
            worst = jnp.maximum(worst, z_bound[h] + carries[h])
        return jnp.max(worst) > EXP_ZERO_BELOW

    zero_c = jnp.zeros((tq, 1), F32)
    zero_a = jnp.zeros((tq, d), F32)
    first = [chunk(h, qi, zero_c, zero_a, True) for h in heads]
    carries = tuple(c for c, _ in first)
    accs = tuple(a for _, a in first)

    def cond(state):
        j, alive, _, _ = state
        return jnp.logical_and(j >= 0, alive)

    def body(state):
        j, _, carries, accs = state
        nxt = [chunk(h, j, carries[h], accs[h], False) for h in heads]
        carries = tuple(c for c, _ in nxt)
        accs = tuple(a for _, a in nxt)
        return j - 1, any_alive(carries), carries, accs

    _, _, _, accs = lax.while_loop(cond, body, (qi - 1, any_alive(carries), carries, accs))
    for h in heads:
        o_ref[0, :, h * d:(h + 1) * d] = accs[h].astype(o_ref.dtype)


def _sb_attention(q, k, v, kn):
    b, h, s, d = q.shape
    nt = kn.shape[2]
    tq = min(TQ_SB, s)
    resident = lambda shape: pl.BlockSpec(shape, lambda i, t: (i,) + (0,) * (len(shape) - 1),
                                          pipeline_mode=pl.Buffered(1))
    return pl.pallas_call(
        functools.partial(_sb_kernel, tq=tq),
        grid=(b, s // tq),
        in_specs=[
            pl.BlockSpec((1, h, tq, d), lambda i, t: (i, 0, t, 0)),
            resident((1, h, s, d)),
            resident((1, h, s, d)),
            resident((1, h, nt, SUBLANES, LANES)),
        ],
        out_specs=pl.BlockSpec((1, tq, h * d), lambda i, t: (i, t, 0)),
        out_shape=jax.ShapeDtypeStruct((b, s, h * d), BF16),
        compiler_params=_cparams("arbitrary", "arbitrary"),
        name="sb_attention",
    )(q, k, v, kn)


def _post_kernel(ym_ref, ys_ref, gm_ref, gs_ref, x_ref, gt_ref, sh_ref, sc_ref, g2_ref,
                 wbm_ref, wbs_ref, wo_ref, x1_ref, h2_ref):
    d = x_ref.shape[-1]
    bm = jnp.dot(ym_ref[0], wbm_ref[...], preferred_element_type=F32)
    bs = jnp.dot(ys_ref[0], wbs_ref[...], preferred_element_type=F32)
    merged = gm_ref[0].astype(F32) * bm + gs_ref[0].astype(F32) * bs
    y = jnp.dot(merged.astype(BF16), wo_ref[...], preferred_element_type=F32)
    x1 = x_ref[0] + gt_ref[0] * y
    x1_ref[0] = x1
    h2 = x1 * _rms(x1, d) * g2_ref[...]
    h2_ref[0] = (h2 * (1.0 + sc_ref[0]) + sh_ref[0]).astype(BF16)


def _post_attention(y_mla, y_sb, gm, gs, x, gt1, sh2, sc2, g2, w_bm, w_bs, w_out):
    b, s, d = x.shape
    tm = min(TM_POST, s)
    tok = lambda w: pl.BlockSpec((1, tm, w), lambda i, j: (i, j, 0))
    per_batch = pl.BlockSpec((1, 1, d), lambda i, j: (i, 0, 0))
    return pl.pallas_call(
        _post_kernel,
        grid=(b, s // tm),
        in_specs=[
            tok(y_mla.shape[-1]), tok(y_sb.shape[-1]), tok(d), tok(d), tok(d),
            per_batch, per_batch, per_batch, _const_spec((1, d)),
            _const_spec(w_bm.shape), _const_spec(w_bs.shape), _const_spec(w_out.shape),
        ],
        out_specs=[tok(d), tok(d)],
        out_shape=[jax.ShapeDtypeStruct((b, s, d), F32), jax.ShapeDtypeStruct((b, s, d), BF16)],
        compiler_params=_cparams("arbitrary", "arbitrary"),
        name="post_attention",
    )(y_mla, y_sb, gm, gs, x, gt1, sh2, sc2, g2, w_bm, w_bs, w_out)


def _ffn_kernel(h_ref, x_ref, gt_ref, wup_ref, wcv_ref, wdn_ref, o_ref, carry_ref, buf_ref, *, tm, fc):
    n_chunks = wdn_ref.shape[0]

    @pl.when(pl.program_id(1) == 0)
    def _():
        carry_ref[...] = jnp.zeros(carry_ref.shape, F32)

    hb = h_ref[0]

    def conv_up(idx):
        u = jnp.dot(hb, wup_ref[idx], preferred_element_type=F32)
        buf_ref[0:SUBLANES, :] = carry_ref[idx]
        buf_ref[SUBLANES:SUBLANES + tm, :] = u
        carry_ref[idx] = u[tm - SUBLANES:tm, :]
        u1 = buf_ref[SUBLANES - 1:SUBLANES - 1 + tm, :]
        u2 = buf_ref[SUBLANES - 2:SUBLANES - 2 + tm, :]
        w = wcv_ref[idx]
        return w[0:1] * u2 + w[1:2] * u1 + w[2:3] * u + w[3:4]

    acc = jnp.zeros((tm, o_ref.shape[-1]), F32)
    for c in range(n_chunks):
        a = conv_up(c)
        v = conv_up(n_chunks + c)
        g = (a * jax.nn.sigmoid(a) * v).astype(BF16)
        acc = acc + jnp.dot(g, wdn_ref[c], preferred_element_type=F32)
    o_ref[0] = x_ref[0] + gt_ref[0] * acc


def _conv_ffn(h2, x1, gt2, w_up, w_cv, w_dn):
    b, s, d = x1.shape
    tm = min(TM_FFN, s)
    n2, _, fc = w_up.shape
    tok = lambda w: pl.BlockSpec((1, tm, w), lambda i, j: (i, j, 0))
    per_batch = pl.BlockSpec((1, 1, d), lambda i, j: (i, 0, 0))
    return pl.pallas_call(
        functools.partial(_ffn_kernel, tm=tm, fc=fc),
        grid=(b, s // tm),
        in_specs=[tok(d), tok(d), per_batch, _const_spec(w_up.shape), _const_spec(w_cv.shape),
                  _const_spec(w_dn.shape)],
        out_specs=tok(d),
        out_shape=jax.ShapeDtypeStruct((b, s, d), F32),
        scratch_shapes=[pltpu.VMEM((n2, SUBLANES, fc), F32), pltpu.VMEM((tm + SUBLANES, fc), F32)],
        compiler_params=_cparams("arbitrary", "arbitrary"),
        name="conv_ffn",
    )(h2, x1, gt2, w_up, w_cv, w_dn)


def _rope_dup_index():
    half = QK_ROPE_DIM // 2
    x1 = np.arange(half)
    x2 = half + np.arange(half)
    return np.concatenate([x1, x2, x2, x1])


def _prep_weights(w_in, b_in, w_uq, w_ukv, g_q_head, g_k_head, w_up, w_conv, b_conv, w_down):
    depth = w_in.shape[0]
    dup = _rope_dup_index()
    rope_lo = Q_LORA_RANK + KV_LORA_RANK
    perm_in = np.concatenate([np.arange(rope_lo), rope_lo + dup,
                              np.arange(rope_lo + QK_ROPE_DIM, w_in.shape[-1])])
    head_idx = np.concatenate([np.arange(QK_NOPE_DIM), QK_NOPE_DIM + dup])
    perm_uq = np.concatenate([h * QK_HEAD_DIM + head_idx for h in range(MLA_HEADS)])
    kv_w = QK_NOPE_DIM + V_HEAD_DIM
    perm_ukv = np.concatenate(
        [h * kv_w + np.arange(QK_NOPE_DIM) for h in range(MLA_HEADS)]
        + [h * kv_w + QK_NOPE_DIM + np.arange(V_HEAD_DIM) for h in range(MLA_HEADS)])
    n_chunks = D_FF // FFN_CHUNK
    d = w_up.shape[1]
    w_up_c = w_up.astype(BF16).reshape(depth, d, 2 * n_chunks, FFN_CHUNK).transpose(0, 2, 1, 3)
    cv = jnp.concatenate([w_conv, b_conv[:, None, :],
                          jnp.zeros((depth, SUBLANES - CONV_WIDTH - 1, 2 * D_FF), F32)], axis=1)
    cv = cv.reshape(depth, SUBLANES, 2 * n_chunks, FFN_CHUNK).transpose(0, 2, 1, 3)
    return dict(
        w_in=w_in[:, :, perm_in].astype(BF16),
        b_in=b_in[:, perm_in].reshape(depth, 1, -1),
        w_uq=w_uq[:, :, perm_uq].astype(BF16),
        w_ukv=w_ukv[:, :, perm_ukv].astype(BF16),
        gq=g_q_head[:, head_idx].reshape(depth, 1, -1),
        gk=g_k_head[:, head_idx].reshape(depth, 1, -1),
        w_up=w_up_c,
        w_cv=cv,
        w_dn=w_down.astype(BF16).reshape(depth, n_chunks, FFN_CHUNK, -1),
    )


def kernel(x, c, positions, w_ada, b_ada, g_norm1, w_in, b_in, g_q_lat, w_uq, g_kv_lat, w_ukv,
           g_q_head, g_k_head, w_branch_mla, w_branch_sb, w_out, g_norm2, w_up, w_conv, b_conv,
           w_down):
    depth = w_in.shape[0]
    b, s, d = x.shape
    cos_t, sin_t = _rope_tables(positions)
    row = lambda g: g.reshape(depth, 1, -1)
    layers = _prep_weights(w_in, b_in, w_uq, w_ukv, g_q_head, g_k_head, w_up, w_conv, b_conv, w_down)
    layers.update(
        mod=_modulation(c, w_ada, b_ada),
        g1=row(g_norm1), gql=row(g_q_lat), gkl=row(g_kv_lat), g2=row(g_norm2),
        w_bm=w_branch_mla.astype(BF16), w_bs=w_branch_sb.astype(BF16), w_o=w_out.astype(BF16))

    def layer(x, p):
        sh1, sc1, gt1, sh2, sc2, gt2 = [p["mod"][:, i * d:(i + 1) * d].reshape(b, 1, d)
                                        for i in range(N_MOD)]
        qm, km, vm, qs, ks, vs, gm, gs, kn = _in_projection(
            x, sh1, sc1, p["g1"], p["w_in"], p["b_in"], p["gql"], p["w_uq"], p["gkl"], p["w_ukv"],
            p["gq"], p["gk"], cos_t, sin_t)
        y_mla = _mla_attention(qm, km, vm)
        y_sb = _sb_attention(qs, ks, vs, kn)
        x1, h2 = _post_attention(y_mla, y_sb, gm, gs, x, gt1, sh2, sc2, p["g2"],
                                 p["w_bm"], p["w_bs"], p["w_o"])
        return _conv_ffn(h2, x1, gt2, p["w_up"], p["w_cv"], p["w_dn"]), None

    x, _ = lax.scan(layer, x, layers)
    return x
```

```python
import functools

import numpy as np
import jax
import jax.numpy as jnp
from jax import lax
from jax.experimental import pallas as pl
from jax.experimental.pallas import tpu as pltpu

F32 = jnp.float32
BF16 = jnp.bfloat16

MLA_HEADS = 4
QK_NOPE_DIM = 128
QK_ROPE_DIM = 64
QK_HEAD_DIM = QK_NOPE_DIM + QK_ROPE_DIM
V_HEAD_DIM = 128
Q_LORA_RANK = 256
KV_LORA_RANK = 128
ROPE_THETA = 10000.0
SB_HEADS = 4
SB_HEAD_DIM = 128
D_FF = 2816
CONV_WIDTH = 3
EPS = 1e-6
N_MOD = 6

LANES = 128
SUBLANES = 8
QK_PAD = 2 * LANES
VMEM_LIMIT = 56 * 1024 * 1024

EXP_ZERO_BELOW = -104.0
NEG_BIG = -1e30
LOG2_E = 1.4426950408889634

TM_PROJ = 512
TM_POST = 512
TM_FFN = 512
FFN_CHUNK = 256
TQ_MLA = 1024
TK_MLA = 256
TQ_SB = 256
TS_ROPE = 2048


def _cparams(*sem):
    return pltpu.CompilerParams(dimension_semantics=sem, vmem_limit_bytes=VMEM_LIMIT)


def _const_spec(shape):
    nd = len(shape)
    return pl.BlockSpec(shape, lambda *_: (0,) * nd, pipeline_mode=pl.Buffered(1))


def _mod_kernel(c_ref, w_ref, b_ref, o_ref):
    c = c_ref[...]
    c_act = c * jax.nn.sigmoid(c)
    o_ref[0] = jnp.dot(c_act, w_ref[0], preferred_element_type=F32) + b_ref[0]


def _modulation(c, w_ada, b_ada):
    depth, d, nd = w_ada.shape
    b = c.shape[0]
    tn = 1024
    return pl.pallas_call(
        _mod_kernel,
        grid=(depth, nd // tn),
        in_specs=[
            pl.BlockSpec((b, d), lambda l, j: (0, 0)),
            pl.BlockSpec((1, d, tn), lambda l, j: (l, 0, j)),
            pl.BlockSpec((1, 1, tn), lambda l, j: (l, 0, j)),
        ],
        out_specs=pl.BlockSpec((1, b, tn), lambda l, j: (l, 0, j)),
        out_shape=jax.ShapeDtypeStruct((depth, b, nd), F32),
        compiler_params=_cparams("arbitrary", "arbitrary"),
        name="adaln_mod",
    )(c, w_ada, b_ada.reshape(depth, 1, nd))


def _rope_table_kernel(pos_ref, freq_ref, cos_ref, sin_ref):
    pos = pos_ref[0].astype(F32)
    ang = pos * freq_ref[...]
    lane = lax.broadcasted_iota(jnp.int32, ang.shape, 1)
    half = QK_ROPE_DIM // 2
    c = jnp.cos(ang)
    s = jnp.sin(ang)
    cos_ref[0] = jnp.where(lane < QK_ROPE_DIM, c, 0.0)
    sin_ref[0] = jnp.where(lane < half, -s, jnp.where(lane < QK_ROPE_DIM, s, 0.0))


def _rope_tables(positions):
    b, s = positions.shape
    ts = min(TS_ROPE, s)
    half = QK_ROPE_DIM // 2
    inv_freq = 1.0 / (ROPE_THETA ** (jnp.arange(half, dtype=F32) * (2.0 / QK_ROPE_DIM)))
    freq = jnp.tile(inv_freq, LANES // half).reshape(1, LANES)
    return pl.pallas_call(
        _rope_table_kernel,
        grid=(b, s // ts),
        in_specs=[
            pl.BlockSpec((1, ts, 1), lambda i, j: (i, j, 0)),
            pl.BlockSpec((1, LANES), lambda i, j: (0, 0)),
        ],
        out_specs=[
            pl.BlockSpec((1, ts, LANES), lambda i, j: (i, j, 0)),
            pl.BlockSpec((1, ts, LANES), lambda i, j: (i, j, 0)),
        ],
        out_shape=[jax.ShapeDtypeStruct((b, s, LANES), F32)] * 2,
        compiler_params=_cparams("arbitrary", "arbitrary"),
        name="rope_tables",
    )(positions.reshape(b, s, 1), freq)


C_QLAT = 0
C_KVLAT = C_QLAT + Q_LORA_RANK
C_KROPE = C_KVLAT + KV_LORA_RANK
C_SBQ = C_KROPE + LANES
C_SBK = C_SBQ + SB_HEADS * SB_HEAD_DIM
C_SBV = C_SBK + SB_HEADS * SB_HEAD_DIM
C_GMLA = C_SBV + SB_HEADS * SB_HEAD_DIM
D_IN_PAD_NO_GATE = C_GMLA


def _rms(v, n):
    return lax.rsqrt(jnp.sum(v * v, axis=-1, keepdims=True) * (1.0 / n) + EPS)


def _inproj_kernel(x_ref, sh_ref, sc_ref, g1_ref, win_ref, bin_ref, gql_ref, wuq_ref, gkl_ref,
                   wukv_ref, gq_ref, gk_ref, cos_ref, sin_ref,
                   qm_ref, km_ref, vm_ref, qs_ref, ks_ref, vs_ref, gm_ref, gs_ref, kn_ref):
    d = x_ref.shape[-1]
    x = x_ref[0]
    h = x * _rms(x, d) * g1_ref[...]
    h = h * (1.0 + sc_ref[0]) + sh_ref[0]
    hb = h.astype(BF16)

    def proj(lo, width):
        return (jnp.dot(hb, win_ref[:, lo:lo + width], preferred_element_type=F32)
                + bin_ref[:, lo:lo + width])

    cos_t = cos_ref[0]
    sin_t = sin_ref[0]

    def rope(v):
        return v * cos_t + pltpu.roll(v, LANES // 2, 1) * sin_t

    lat = proj(C_QLAT, C_SBQ)
    q_lat = lat[:, C_QLAT:C_KVLAT]
    kv_lat = lat[:, C_KVLAT:C_KROPE]
    k_rope = lat[:, C_KROPE:C_SBQ]

    qn = (q_lat * _rms(q_lat, Q_LORA_RANK) * gql_ref[...]).astype(BF16)
    q = jnp.dot(qn, wuq_ref[...], preferred_element_type=F32)
    gq = gq_ref[...]
    qk_scale = QK_HEAD_DIM ** -0.5 * LOG2_E
    for hd in range(MLA_HEADS):
        nope = q[:, hd * QK_PAD:hd * QK_PAD + LANES]
        rv = q[:, hd * QK_PAD + LANES:(hd + 1) * QK_PAD]
        ss = (jnp.sum(nope * nope, axis=-1, keepdims=True)
              + 0.5 * jnp.sum(rv * rv, axis=-1, keepdims=True))
        r = lax.rsqrt(ss * (1.0 / QK_HEAD_DIM) + EPS)
        qm_ref[0, hd, :, 0:LANES] = (nope * r * gq[:, 0:LANES] * qk_scale).astype(BF16)
        qm_ref[0, hd, :, LANES:QK_PAD] = (rope(rv * r * gq[:, LANES:QK_PAD]) * qk_scale).astype(BF16)

    kvn = (kv_lat * _rms(kv_lat, KV_LORA_RANK) * gkl_ref[...]).astype(BF16)
    kv = jnp.dot(kvn, wukv_ref[...], preferred_element_type=F32)
    gk = gk_ref[...]
    kr_rot = rope(k_rope * gk[:, LANES:QK_PAD])
    kr_ss = 0.5 * jnp.sum(k_rope * k_rope, axis=-1, keepdims=True)
    v_off = MLA_HEADS * QK_NOPE_DIM
    for hd in range(MLA_HEADS):
        kn = kv[:, hd * LANES:(hd + 1) * LANES]
        ss = jnp.sum(kn * kn, axis=-1, keepdims=True) + kr_ss
        r = lax.rsqrt(ss * (1.0 / QK_HEAD_DIM) + EPS)
        km_ref[0, hd, :, 0:LANES] = (kn * r * gk[:, 0:LANES]).astype(BF16)
        km_ref[0, hd, :, LANES:QK_PAD] = (kr_rot * r).astype(BF16)
        vm_ref[0, hd] = kv[:, v_off + hd * LANES:v_off + (hd + 1) * LANES].astype(BF16)

    sb_scale = SB_HEAD_DIM ** -0.5
    for hd in range(SB_HEADS):
        qs_ref[0, hd] = (proj(C_SBQ + hd * LANES, LANES) * sb_scale).astype(BF16)
        kb = proj(C_SBK + hd * LANES, LANES).astype(BF16)
        ks_ref[0, hd] = kb
        kf = kb.astype(F32)
        n2 = jnp.max(jnp.sum(kf * kf, axis=-1, keepdims=True), axis=0, keepdims=True)
        kn_ref[0, hd, 0] = jnp.broadcast_to(n2, (SUBLANES, LANES))
        vs_ref[0, hd] = proj(C_SBV + hd * LANES, LANES).astype(BF16)

    gm_ref[0] = jax.nn.sigmoid(proj(C_GMLA, d)).astype(BF16)
    gs_ref[0] = jax.nn.sigmoid(proj(C_GMLA + d, d)).astype(BF16)


def _in_projection(x, sh1, sc1, g1, w_in, b_in, g_q_lat, w_uq, g_kv_lat, w_ukv, gq, gk, cos_t, sin_t):
    b, s, d = x.shape
    tm = min(TM_PROJ, s)
    nt = s // tm
    d_in = w_in.shape[1]
    tok = lambda w: pl.BlockSpec((1, tm, w), lambda i, j: (i, j, 0))
    per_batch = pl.BlockSpec((1, 1, d), lambda i, j: (i, 0, 0))
    head = lambda h, w: pl.BlockSpec((1, h, tm, w), lambda i, j: (i, 0, j, 0))
    out_shape = [
        jax.ShapeDtypeStruct((b, MLA_HEADS, s, QK_PAD), BF16),
        jax.ShapeDtypeStruct((b, MLA_HEADS, s, QK_PAD), BF16),
        jax.ShapeDtypeStruct((b, MLA_HEADS, s, V_HEAD_DIM), BF16),
        jax.ShapeDtypeStruct((b, SB_HEADS, s, SB_HEAD_DIM), BF16),
        jax.ShapeDtypeStruct((b, SB_HEADS, s, SB_HEAD_DIM), BF16),
        jax.ShapeDtypeStruct((b, SB_HEADS, s, SB_HEAD_DIM), BF16),
        jax.ShapeDtypeStruct((b, s, d), BF16),
        jax.ShapeDtypeStruct((b, s, d), BF16),
        jax.ShapeDtypeStruct((b, SB_HEADS, nt, SUBLANES, LANES), F32),
    ]
    out_specs = [
        head(MLA_HEADS, QK_PAD), head(MLA_HEADS, QK_PAD), head(MLA_HEADS, V_HEAD_DIM),
        head(SB_HEADS, SB_HEAD_DIM), head(SB_HEADS, SB_HEAD_DIM), head(SB_HEADS, SB_HEAD_DIM),
        tok(d), tok(d),
        pl.BlockSpec((1, SB_HEADS, 1, SUBLANES, LANES), lambda i, j: (i, 0, j, 0, 0)),
    ]
    in_specs = [
        tok(d), per_batch, per_batch, _const_spec((1, d)),
        _const_spec((d, d_in)), _const_spec((1, d_in)),
        _const_spec((1, Q_LORA_RANK)), _const_spec(w_uq.shape),
        _const_spec((1, KV_LORA_RANK)), _const_spec(w_ukv.shape),
        _const_spec((1, QK_PAD)), _const_spec((1, QK_PAD)),
        tok(LANES), tok(LANES),
    ]
    return pl.pallas_call(
        _inproj_kernel,
        grid=(b, nt),
        in_specs=in_specs,
        out_specs=out_specs,
        out_shape=out_shape,
        compiler_params=_cparams("arbitrary", "arbitrary"),
        name="in_projection",
    )(x, sh1, sc1, g1, w_in, b_in, g_q_lat, w_uq, g_kv_lat, w_ukv, gq, gk, cos_t, sin_t)


def _mla_kernel(q_ref, k_ref, v_ref, o_ref, m_ref, l_ref, acc_ref, s_ref, *, tq, tk):
    qi = pl.program_id(2)
    per_q = tq // tk
    n_full = qi * per_q

    def scores(kc, r0):
        start = pl.multiple_of(kc * tk, tk)
        k = k_ref[0, 0, pl.ds(start, tk), :]
        return lax.dot_general(q_ref[0, 0, r0:tq, :], k, (((1,), (1,)), ((), ())),
                               preferred_element_type=F32)

    def update(kc, s, r0, masked):
        start = pl.multiple_of(kc * tk, tk)
        v = v_ref[0, 0, pl.ds(start, tk), :]
        if masked:
            row = r0 + lax.broadcasted_iota(jnp.int32, s.shape, 0)
            col = (start - qi * tq) + lax.broadcasted_iota(jnp.int32, s.shape, 1)
            s = jnp.where(col <= row, s, NEG_BIG)
        m = m_ref[r0:tq, :]
        m_new = jnp.maximum(m, jnp.max(s, axis=1, keepdims=True))
        alpha = jnp.exp2(m - m_new)
        p = jnp.exp2(s - jnp.concatenate([m_new] * (tk // LANES), axis=1))
        m_ref[r0:tq, :] = m_new
        l_ref[r0:tq, :] = alpha * l_ref[r0:tq, :] + (p[:, :LANES] + p[:, LANES:])
        acc_ref[r0:tq, :] = alpha * acc_ref[r0:tq, :] + jnp.dot(
            p.astype(BF16), v, preferred_element_type=F32)

    m_ref[...] = jnp.full(m_ref.shape, NEG_BIG, F32)
    l_ref[...] = jnp.zeros(l_ref.shape, F32)
    acc_ref[...] = jnp.zeros(acc_ref.shape, F32)
    s_ref[...] = scores(0, 0)

    def body(t, _):
        c = per_q * t
        s_cur = s_ref[...]
        for u in range(per_q):
            s_nxt = scores(c + u + 1, 0)
            update(c + u, s_cur, 0, False)
            s_cur = s_nxt
        s_ref[...] = s_cur
        return 0

    lax.fori_loop(0, qi, body, 0)
    update(n_full, s_ref[...], 0, True)
    for j in range(1, per_q):
        update(n_full + j, scores(n_full + j, j * tk), j * tk, True)
    l = jnp.sum(l_ref[...], axis=1, keepdims=True)
    o_ref[0] = (acc_ref[...] / l).astype(o_ref.dtype)


def _mla_attention(q, k, v):
    b, h, s, _ = q.shape
    dv = v.shape[-1]
    tq = min(TQ_MLA, s)
    tk = min(TK_MLA, tq)
    return pl.pallas_call(
        functools.partial(_mla_kernel, tq=tq, tk=tk),
        grid=(b, h, s // tq),
        in_specs=[
            pl.BlockSpec((1, 1, tq, QK_PAD), lambda i, j, t: (i, j, t, 0)),
            pl.BlockSpec((1, 1, s, QK_PAD), lambda i, j, t: (i, j, 0, 0)),
            pl.BlockSpec((1, 1, s, dv), lambda i, j, t: (i, j, 0, 0)),
        ],
        out_specs=pl.BlockSpec((1, tq, dv), lambda i, j, t: (i, t, j)),
        out_shape=jax.ShapeDtypeStruct((b, s, h * dv), BF16),
        scratch_shapes=[pltpu.VMEM((tq, LANES), F32), pltpu.VMEM((tq, LANES), F32),
                        pltpu.VMEM((tq, dv), F32), pltpu.VMEM((tq, tk), F32)],
        compiler_params=_cparams("arbitrary", "arbitrary", "arbitrary"),
        name="mla_attention",
    )(q, k, v)


def _sb_kernel(q_ref, k_ref, v_ref, kn_ref, o_ref, *, tq):
    qi = pl.program_id(1)
    nh, d = q_ref.shape[1], q_ref.shape[-1]
    heads = range(nh)
    qs = [q_ref[0, h] for h in heads]
    z_bound = []
    for h in heads:
        qf = qs[h].astype(F32)
        q_norm = jnp.sqrt(jnp.sum(qf * qf, axis=-1, keepdims=True))
        k_norm = jnp.sqrt(jnp.max(kn_ref[0, h]))
        z_bound.append(q_norm * (k_norm * 1.001) + 1e-3)

    tri = (lax.broadcasted_iota(jnp.int32, (tq, tq), 0)
           >= lax.broadcasted_iota(jnp.int32, (tq, tq), 1)).astype(BF16)

    def chunks(j, carries, accs, masked):
        start = pl.multiple_of(j * tq, tq)
        if masked:
            past = (lax.broadcasted_iota(jnp.int32, (tq, tq), 1)
                    < lax.broadcasted_iota(jnp.int32, (tq, tq), 0))
        zs, lss, his, los = [], [], [], []
        for h in heads:
            k = k_ref[0, h, pl.ds(start, tq), :]
            z = lax.dot_general(qs[h], k, (((1,), (1,)), ((), ())), preferred_element_type=F32)
            ls = jnp.minimum(-z, 0.0) - jnp.log(1.0 + jnp.exp(-jnp.abs(z)))
            if masked:
                ls = jnp.where(past, ls, 0.0)
            hi = ls.astype(BF16)
            zs.append(z)
            lss.append(ls)
            his.append(hi)
            los.append((ls - hi.astype(F32)).astype(BF16))
        rc_all = jnp.dot(jnp.concatenate(his + los, axis=0), tri, preferred_element_type=F32)
        new_c, new_a = [], []
        for h in heads:
            rc = rc_all[h * tq:(h + 1) * tq] + rc_all[(nh + h) * tq:(nh + h + 1) * tq]
            a = jnp.exp(zs[h] + rc + carries[h])
            if masked:
                a = jnp.where(past, a, 0.0)
            v = v_ref[0, h, pl.ds(start, tq), :]
            new_a.append(accs[h] + jnp.dot(a.astype(BF16), v, preferred_element_type=F32))
            new_c.append(carries[h] + jnp.sum(lss[h], axis=1, keepdims=True))
        return tuple(new_c), tuple(new_a)

    def any_alive(carries):
        worst = z_bound[0] + carries[0]
        for h in heads[1:]:
            worst = jnp.maximum(worst, z_bound[h] + carries[h])
        return jnp.max(worst) > EXP_ZERO_BELOW

    carries, accs = chunks(qi, (jnp.zeros((tq, 1), F32),) * nh, (jnp.zeros((tq, d), F32),) * nh, True)

    def cond(state):
        j, alive, _, _ = state
        return jnp.logical_and(j >= 0, alive)

    def body(state):
        j, _, carries, accs = state
        carries, accs = chunks(j, carries, accs, False)
        return j - 1, any_alive(carries), carries, accs

    _, _, _, accs = lax.while_loop(cond, body, (qi - 1, any_alive(carries), carries, accs))
    for h in heads:
        o_ref[0, :, h * d:(h + 1) * d] = accs[h].astype(o_ref.dtype)


def _sb_attention(q, k, v, kn):
    b, h, s, d = q.shape
    nt = kn.shape[2]
    tq = min(TQ_SB, s)
    resident = lambda shape: pl.BlockSpec(shape, lambda i, t: (i,) + (0,) * (len(shape) - 1),
                                          pipeline_mode=pl.Buffered(1))
    return pl.pallas_call(
        functools.partial(_sb_kernel, tq=tq),
        grid=(b, s // tq),
        in_specs=[
            pl.BlockSpec((1, h, tq, d), lambda i, t: (i, 0, t, 0)),
            resident((1, h, s, d)),
            resident((1, h, s, d)),
            resident((1, h, nt, SUBLANES, LANES)),
        ],
        out_specs=pl.BlockSpec((1, tq, h * d), lambda i, t: (i, t, 0)),
        out_shape=jax.ShapeDtypeStruct((b, s, h * d), BF16),
        compiler_params=_cparams("arbitrary", "arbitrary"),
        name="sb_attention",
    )(q, k, v, kn)


def _post_kernel(ym_ref, ys_ref, gm_ref, gs_ref, x_ref, gt_ref, sh_ref, sc_ref, g2_ref,
                 wbm_ref, wbs_ref, wo_ref, x1_ref, h2_ref):
    d = x_ref.shape[-1]
    bm = jnp.dot(ym_ref[0], wbm_ref[...], preferred_element_type=F32)
    bs = jnp.dot(ys_ref[0], wbs_ref[...], preferred_element_type=F32)
    merged = gm_ref[0].astype(F32) * bm + gs_ref[0].astype(F32) * bs
    y = jnp.dot(merged.astype(BF16), wo_ref[...], preferred_element_type=F32)
    x1 = x_ref[0] + gt_ref[0] * y
    x1_ref[0] = x1
    h2 = x1 * _rms(x1, d) * g2_ref[...]
    h2_ref[0] = (h2 * (1.0 + sc_ref[0]) + sh_ref[0]).astype(BF16)


def _post_attention(y_mla, y_sb, gm, gs, x, gt1, sh2, sc2, g2, w_bm, w_bs, w_out):
    b, s, d = x.shape
    tm = min(TM_POST, s)
    tok = lambda w: pl.BlockSpec((1, tm, w), lambda i, j: (i, j, 0))
    per_batch = pl.BlockSpec((1, 1, d), lambda i, j: (i, 0, 0))
    return pl.pallas_call(
        _post_kernel,
        grid=(b, s // tm),
        in_specs=[
            tok(y_mla.shape[-1]), tok(y_sb.shape[-1]), tok(d), tok(d), tok(d),
            per_batch, per_batch, per_batch, _const_spec((1, d)),
            _const_spec(w_bm.shape), _const_spec(w_bs.shape), _const_spec(w_out.shape),
        ],
        out_specs=[tok(d), tok(d)],
        out_shape=[jax.ShapeDtypeStruct((b, s, d), F32), jax.ShapeDtypeStruct((b, s, d), BF16)],
        compiler_params=_cparams("arbitrary", "arbitrary"),
        name="post_attention",
    )(y_mla, y_sb, gm, gs, x, gt1, sh2, sc2, g2, w_bm, w_bs, w_out)


def _ffn_kernel(h_ref, x_ref, gt_ref, wup_ref, wcv_ref, wdn_ref, o_ref, carry_ref, buf_ref, *, tm, fc):
    n_chunks = wdn_ref.shape[0]

    @pl.when(pl.program_id(1) == 0)
    def _():
        carry_ref[...] = jnp.zeros(carry_ref.shape, F32)

    hb = h_ref[0]

    def conv_up(idx):
        u = jnp.dot(hb, wup_ref[idx], preferred_element_type=F32)
        buf_ref[0:SUBLANES, :] = carry_ref[idx]
        buf_ref[SUBLANES:SUBLANES + tm, :] = u
        carry_ref[idx] = u[tm - SUBLANES:tm, :]
        u1 = buf_ref[SUBLANES - 1:SUBLANES - 1 + tm, :]
        u2 = buf_ref[SUBLANES - 2:SUBLANES - 2 + tm, :]
        w = wcv_ref[idx]
        return w[0:1] * u2 + w[1:2] * u1 + w[2:3] * u + w[3:4]

    acc = jnp.zeros((tm, o_ref.shape[-1]), F32)
    for c in range(n_chunks):
        a = conv_up(c)
        v = conv_up(n_chunks + c)
        g = (a * jax.nn.sigmoid(a) * v).astype(BF16)
        acc = acc + jnp.dot(g, wdn_ref[c], preferred_element_type=F32)
    o_ref[0] = x_ref[0] + gt_ref[0] * acc


def _conv_ffn(h2, x1, gt2, w_up, w_cv, w_dn):
    b, s, d = x1.shape
    tm = min(TM_FFN, s)
    n2, _, fc = w_up.shape
    tok = lambda w: pl.BlockSpec((1, tm, w), lambda i, j: (i, j, 0))
    per_batch = pl.BlockSpec((1, 1, d), lambda i, j: (i, 0, 0))
    return pl.pallas_call(
        functools.partial(_ffn_kernel, tm=tm, fc=fc),
        grid=(b, s // tm),
        in_specs=[tok(d), tok(d), per_batch, _const_spec(w_up.shape), _const_spec(w_cv.shape),
                  _const_spec(w_dn.shape)],
        out_specs=tok(d),
        out_shape=jax.ShapeDtypeStruct((b, s, d), F32),
        scratch_shapes=[pltpu.VMEM((n2, SUBLANES, fc), F32), pltpu.VMEM((tm + SUBLANES, fc), F32)],
        compiler_params=_cparams("arbitrary", "arbitrary"),
        name="conv_ffn",
    )(h2, x1, gt2, w_up, w_cv, w_dn)


def _rope_dup_index():
    half = QK_ROPE_DIM // 2
    x1 = np.arange(half)
    x2 = half + np.arange(half)
    return np.concatenate([x1, x2, x2, x1])


def _prep_weights(w_in, b_in, w_uq, w_ukv, g_q_head, g_k_head, w_up, w_conv, b_conv, w_down):
    depth = w_in.shape[0]
    dup = _rope_dup_index()
    rope_lo = Q_LORA_RANK + KV_LORA_RANK
    perm_in = np.concatenate([np.arange(rope_lo), rope_lo + dup,
                              np.arange(rope_lo + QK_ROPE_DIM, w_in.shape[-1])])
    head_idx = np.concatenate([np.arange(QK_NOPE_DIM), QK_NOPE_DIM + dup])
    perm_uq = np.concatenate([h * QK_HEAD_DIM + head_idx for h in range(MLA_HEADS)])
    kv_w = QK_NOPE_DIM + V_HEAD_DIM
    perm_ukv = np.concatenate(
        [h * kv_w + np.arange(QK_NOPE_DIM) for h in range(MLA_HEADS)]
        + [h * kv_w + QK_NOPE_DIM + np.arange(V_HEAD_DIM) for h in range(MLA_HEADS)])
    n_chunks = D_FF // FFN_CHUNK
    d = w_up.shape[1]
    w_up_c = w_up.astype(BF16).reshape(depth, d, 2 * n_chunks, FFN_CHUNK).transpose(0, 2, 1, 3)
    cv = jnp.concatenate([w_conv, b_conv[:, None, :],
                          jnp.zeros((depth, SUBLANES - CONV_WIDTH - 1, 2 * D_FF), F32)], axis=1)
    cv = cv.reshape(depth, SUBLANES, 2 * n_chunks, FFN_CHUNK).transpose(0, 2, 1, 3)
    return dict(
        w_in=w_in[:, :, perm_in].astype(BF16),
        b_in=b_in[:, perm_in].reshape(depth, 1, -1),
        w_uq=w_uq[:, :, perm_uq].astype(BF16),
        w_ukv=w_ukv[:, :, perm_ukv].astype(BF16),
        gq=g_q_head[:, head_idx].reshape(depth, 1, -1),
        gk=g_k_head[:, head_idx].reshape(depth, 1, -1),
        w_up=w_up_c,
        w_cv=cv,
        w_dn=w_down.astype(BF16).reshape(depth, n_chunks, FFN_CHUNK, -1),
    )


def kernel(x, c, positions, w_ada, b_ada, g_norm1, w_in, b_in, g_q_lat, w_uq, g_kv_lat, w_ukv,
           g_q_head, g_k_head, w_branch_mla, w_branch_sb, w_out, g_norm2, w_up, w_conv, b_conv,
           w_down):
    depth = w_in.shape[0]
    b, s, d = x.shape
    cos_t, sin_t = _rope_tables(positions)
    row = lambda g: g.reshape(depth, 1, -1)
    layers = _prep_weights(w_in, b_in, w_uq, w_ukv, g_q_head, g_k_head, w_up, w_conv, b_conv, w_down)
    layers.update(
        mod=_modulation(c, w_ada, b_ada),
        g1=row(g_norm1), gql=row(g_q_lat), gkl=row(g_kv_lat), g2=row(g_norm2),
        w_bm=w_branch_mla.astype(BF16), w_bs=w_branch_sb.astype(BF16), w_o=w_out.astype(BF16))

    def layer(x, p):
        sh1, sc1, gt1, sh2, sc2, gt2 = [p["mod"][:, i * d:(i + 1) * d].reshape(b, 1, d)
                                        for i in range(N_MOD)]
        qm, km, vm, qs, ks, vs, gm, gs, kn = _in_projection(
            x, sh1, sc1, p["g1"], p["w_in"], p["b_in"], p["gql"], p["w_uq"], p["gkl"], p["w_ukv"],
            p["gq"], p["gk"], cos_t, sin_t)
        y_mla = _mla_attention(qm, km, vm)
        y_sb = _sb_attention(qs, ks, vs, kn)
        x1, h2 = _post_attention(y_mla, y_sb, gm, gs, x, gt1, sh2, sc2, p["g2"],
                                 p["w_bm"], p["w_bs"], p["w_o"])
        return _conv_ffn(h2, x1, gt2, p["w_up"], p["w_cv"], p["w_dn"]), None

    x, _ = lax.scan(layer, x, layers)
    return x
```
